```python
import math
import jax, jax.numpy as jnp
from jax import lax
import numpy as np

D_MODEL = 1024
BATCH = 4
SEQ = 4096
DEPTH = 4

HEAD_DIM = 64
V_DIM = 2 * HEAD_DIM
ATTN_W = D_MODEL // 2
N_HEADS = ATTN_W // V_DIM
QK_W = N_HEADS * 2 * HEAD_DIM
CONV_W = D_MODEL // 2
CONV_K = 3
N_CONV_GROUPS = 8
D_FF = (((8 * D_MODEL) // 3 + 255) // 256) * 256
NUM_BUCKETS = 32
MAX_EXACT = NUM_BUCKETS // 2
MAX_DISTANCE = 128
Q_BLOCK = 128
NORM_EPS = 1e-6
SUBLN_EPS = 1e-5
COL_SIZES = (QK_W, QK_W, ATTN_W, CONV_W, CONV_W, CONV_W, D_MODEL, D_MODEL)
IN_COLS = sum(COL_SIZES)

kernel_name = "hybrid_diffattn_shortconv_macaron"


def rmsnorm(x, g, eps=NORM_EPS):
    xf = x.astype(jnp.float32)
    y = xf * lax.rsqrt(jnp.mean(xf * xf, axis=-1, keepdims=True) + eps)
    return (y * g.astype(jnp.float32)).astype(x.dtype)


def swiglu(h, w_gate, w_up, w_down):
    return (jax.nn.silu(h @ w_gate) * (h @ w_up)) @ w_down


def t5_causal_bucket(dist):
    n = jnp.maximum(dist, 0)
    is_small = n < MAX_EXACT
    nf = jnp.maximum(n, MAX_EXACT).astype(jnp.float32)
    large = MAX_EXACT + (jnp.log(nf / MAX_EXACT) / math.log(MAX_DISTANCE / MAX_EXACT)
                         * (NUM_BUCKETS - MAX_EXACT)).astype(jnp.int32)
    large = jnp.minimum(large, NUM_BUCKETS - 1)
    return jnp.where(is_small, n, large)


def diff_attention(q1, q2, k1, k2, v, rel_bias, lam):
    T = q1.shape[2]
    n_blocks = T // Q_BLOCK
    k_pos = jnp.arange(T, dtype=jnp.int32)

    def one_block(i):
        start = i * Q_BLOCK
        q1b = lax.dynamic_slice_in_dim(q1, start, Q_BLOCK, axis=2)
        q2b = lax.dynamic_slice_in_dim(q2, start, Q_BLOCK, axis=2)
        q_pos = start + jnp.arange(Q_BLOCK, dtype=jnp.int32)
        dist = q_pos[:, None] - k_pos[None, :]
        bias = jnp.transpose(rel_bias.astype(jnp.float32)[t5_causal_bucket(dist)], (2, 0, 1))
        visible = dist >= 0
        s1 = jnp.einsum('bhqd,bhkd->bhqk', q1b, k1).astype(jnp.float32) + bias
        s2 = jnp.einsum('bhqd,bhkd->bhqk', q2b, k2).astype(jnp.float32) + bias
        p1 = jax.nn.softmax(jnp.where(visible, s1, -jnp.inf), axis=-1)
        p2 = jax.nn.softmax(jnp.where(visible, s2, -jnp.inf), axis=-1)
        p = (p1 - lam * p2).astype(v.dtype)
        return jnp.einsum('bhqk,bhkd->bhqd', p, v)

    out = lax.map(one_block, jnp.arange(n_blocks, dtype=jnp.int32))
    nb, b, h, qb, dv = out.shape
    return jnp.transpose(out, (1, 0, 3, 2, 4)).reshape(b, nb * qb, h, dv)


def causal_depthwise_conv(u, w):
    c = u.shape[-1]
    return lax.conv_general_dilated(
        u, w[:, None, :].astype(u.dtype), window_strides=(1,), padding=[(CONV_K - 1, 0)],
        dimension_numbers=('NWC', 'WIO', 'NWC'), feature_group_count=c)


def setup_inputs(seed: int = 0) -> dict:
    key = jax.random.key(seed)
    ks = jax.random.split(key, 24)
    nrm = lambda k, shape, fan_in, s=1.0: jax.random.normal(k, shape, jnp.float32) * (s * fan_in ** -0.5)
    gain = lambda k, shape: 1.0 + 0.05 * jax.random.normal(k, shape, jnp.float32)
    return {
        "x": jax.random.normal(ks[0], (BATCH, SEQ, D_MODEL), jnp.float32),
        "norm_ffn1_g": gain(ks[1], (DEPTH, D_MODEL)),
        "ffn1_w_gate": nrm(ks[2], (DEPTH, D_MODEL, D_FF), D_MODEL),
        "ffn1_w_up": nrm(ks[3], (DEPTH, D_MODEL, D_FF), D_MODEL),
        "ffn1_w_down": nrm(ks[4], (DEPTH, D_FF, D_MODEL), D_FF),
        "norm_mix_g": gain(ks[5], (DEPTH, D_MODEL)),
        "w_in": nrm(ks[6], (DEPTH, D_MODEL, IN_COLS), D_MODEL),
        "lambda_q1": 0.1 * jax.random.normal(ks[7], (DEPTH, HEAD_DIM), jnp.float32),
        "lambda_k1": 0.1 * jax.random.normal(ks[8], (DEPTH, HEAD_DIM), jnp.float32),
        "lambda_q2": 0.1 * jax.random.normal(ks[9], (DEPTH, HEAD_DIM), jnp.float32),
        "lambda_k2": 0.1 * jax.random.normal(ks[10], (DEPTH, HEAD_DIM), jnp.float32),
        "subln_g": gain(ks[11], (DEPTH, V_DIM)),
        "rel_bias": 0.5 * jax.random.normal(ks[12], (NUM_BUCKETS, N_HEADS), jnp.float32),
        "conv_w": nrm(ks[13], (DEPTH, CONV_K, CONV_W), CONV_K),
        "w_branch_attn": nrm(ks[14], (DEPTH, ATTN_W, D_MODEL), ATTN_W),
        "w_branch_conv": nrm(ks[15], (DEPTH, CONV_W, D_MODEL), CONV_W),
        "w_out": nrm(ks[16], (DEPTH, D_MODEL, D_MODEL), D_MODEL),
        "norm_ffn2_g": gain(ks[17], (DEPTH, D_MODEL)),
        "ffn2_w_gate": nrm(ks[18], (DEPTH, D_MODEL, D_FF), D_MODEL),
        "ffn2_w_up": nrm(ks[19], (DEPTH, D_MODEL, D_FF), D_MODEL),
        "ffn2_w_down": nrm(ks[20], (DEPTH, D_FF, D_MODEL), D_FF),
        "final_g": gain(ks[21], (D_MODEL,)),
    }


def reference(x, norm_ffn1_g, ffn1_w_gate, ffn1_w_up, ffn1_w_down, norm_mix_g, w_in,
              lambda_q1, lambda_k1, lambda_q2, lambda_k2, subln_g, rel_bias, conv_w,
              w_branch_attn, w_branch_conv, w_out, norm_ffn2_g, ffn2_w_gate, ffn2_w_up,
              ffn2_w_down, final_g):
    B, T, _ = x.shape
    split_idx = list(np.cumsum(COL_SIZES)[:-1])
    q_scale = HEAD_DIM ** -0.5
    for l in range(DEPTH):
        lambda_init = 0.8 - 0.6 * math.exp(-0.3 * l)

        x = x + 0.5 * swiglu(rmsnorm(x, norm_ffn1_g[l]), ffn1_w_gate[l], ffn1_w_up[l], ffn1_w_down[l])

        h = rmsnorm(x, norm_mix_g[l])
        proj = h @ w_in[l]
        q, k, v, c_b, c_c, c_u, g_attn, g_conv = jnp.split(proj, split_idx, axis=-1)

        q = (q * q_scale).reshape(B, T, N_HEADS, 2, HEAD_DIM)
        k = k.reshape(B, T, N_HEADS, 2, HEAD_DIM)
        to_bhtd = lambda a: jnp.transpose(a, (0, 2, 1, 3))
        q1, q2 = to_bhtd(q[..., 0, :]), to_bhtd(q[..., 1, :])
        k1, k2 = to_bhtd(k[..., 0, :]), to_bhtd(k[..., 1, :])
        vh = to_bhtd(v.reshape(B, T, N_HEADS, V_DIM))
        lam = (jnp.exp(jnp.sum(lambda_q1[l].astype(jnp.float32) * lambda_k1[l].astype(jnp.float32)))
               - jnp.exp(jnp.sum(lambda_q2[l].astype(jnp.float32) * lambda_k2[l].astype(jnp.float32)))
               + lambda_init)
        attn = diff_attention(q1, q2, k1, k2, vh, rel_bias, lam)
        attn = rmsnorm(attn, subln_g[l], SUBLN_EPS) * (1.0 - lambda_init)
        y_attn = attn.reshape(B, T, ATTN_W) @ w_branch_attn[l]

        y_conv = (c_b * causal_depthwise_conv(c_c * c_u, conv_w[l])) @ w_branch_conv[l]

        merged = jax.nn.sigmoid(g_attn) * y_attn + jax.nn.sigmoid(g_conv) * y_conv
        x = x + merged @ w_out[l]

        x = x + 0.5 * swiglu(rmsnorm(x, norm_ffn2_g[l]), ffn2_w_gate[l], ffn2_w_up[l], ffn2_w_down[l])
    return rmsnorm(x, final_g)
```

```python
import functools
import math

import jax
import jax.numpy as jnp
from jax import lax
from jax.experimental import pallas as pl
from jax.experimental.pallas import tpu as pltpu

D_MODEL = 1024
HEAD_DIM = 64
V_DIM = 2 * HEAD_DIM
ATTN_W = D_MODEL // 2
N_HEADS = ATTN_W // V_DIM
CONV_W = D_MODEL // 2
CONV_K = 3
D_FF = (((8 * D_MODEL) // 3 + 255) // 256) * 256
NUM_BUCKETS = 32
MAX_EXACT = NUM_BUCKETS // 2
MAX_DISTANCE = 128
NORM_EPS = 1e-6
SUBLN_EPS = 1e-5

LOG2E = math.log2(math.e)
Q_SCALE = HEAD_DIM ** -0.5 * LOG2E
MASK_VALUE = -1e30

V7X_SUBLANES = 8
MIB = 1 << 20

FFN_TM = 512
FFN_TF = D_FF // 2
MIX_TM = 512
ATT_T = 512


def _bucket_upper_bounds():
    def bucket(n):
        if n < MAX_EXACT:
            return n
        v = math.log(n / MAX_EXACT) / math.log(MAX_DISTANCE / MAX_EXACT)
        return min(MAX_EXACT + int(v * (NUM_BUCKETS - MAX_EXACT)), NUM_BUCKETS - 1)

    bounds = []
    n = 0
    for b in range(NUM_BUCKETS - 1):
        while bucket(n) <= b:
            n += 1
        bounds.append(n)
    return tuple(bounds)


BUCKET_UPPER = _bucket_upper_bounds()
BIAS_CONST_FROM = BUCKET_UPPER[-1]


def _rms_scale(x, g, eps):
    ms = jnp.mean(x * x, axis=-1, keepdims=True)
    return x * lax.rsqrt(ms + eps) * g


def _vmem_limit(nbytes):
    return int(-(-nbytes // MIB) * MIB)


def _ffn_kernel(x_ref, g_ref, wg_ref, wu_ref, wd_ref, o_ref, h_ref, acc_ref):
    j = pl.program_id(1)

    @pl.when(j == 0)
    def _():
        h_ref[...] = _rms_scale(x_ref[...], g_ref[...], NORM_EPS).astype(jnp.bfloat16)
        acc_ref[...] = jnp.zeros_like(acc_ref)

    h = h_ref[...]
    gate = jnp.dot(h, wg_ref[...], preferred_element_type=jnp.float32)
    up = jnp.dot(h, wu_ref[...], preferred_element_type=jnp.float32)
    act = (gate * jax.nn.sigmoid(gate) * up).astype(jnp.bfloat16)
    acc_ref[...] += jnp.dot(act, wd_ref[...], preferred_element_type=jnp.float32)

    @pl.when(j == pl.num_programs(1) - 1)
    def _():
        o_ref[...] = x_ref[...] + 0.5 * acc_ref[...]


def _ffn(x, g, wg, wu, wd, layer):
    m, d = x.shape
    f = wg.shape[-1]
    tm, tf = FFN_TM, FFN_TF
    est = (4 * tm * d * 4 + tm * d * 4 + tm * d * 2
           + 2 * 3 * d * tf * 2
           + 2 * tm * tf * 4 + tm * tf * 2 + tm * d * 4)
    return pl.pallas_call(
        _ffn_kernel,
        grid=(m // tm, f // tf),
        in_specs=[
            pl.BlockSpec((tm, d), lambda i, j: (i, 0)),
            pl.BlockSpec((None, 1, d), lambda i, j: (layer, 0, 0)),
            pl.BlockSpec((None, d, tf), lambda i, j: (layer, 0, j)),
            pl.BlockSpec((None, d, tf), lambda i, j: (layer, 0, j)),
            pl.BlockSpec((None, tf, d), lambda i, j: (layer, j, 0)),
        ],
        out_specs=pl.BlockSpec((tm, d), lambda i, j: (i, 0)),
        out_shape=jax.ShapeDtypeStruct((m, d), jnp.float32),
        scratch_shapes=[pltpu.VMEM((tm, d), jnp.bfloat16), pltpu.VMEM((tm, d), jnp.float32)],
        compiler_params=pltpu.CompilerParams(
            dimension_semantics=("arbitrary", "arbitrary"),
            vmem_limit_bytes=_vmem_limit(est)),
        name="ffn",
    )(x, g, wg, wu, wd)


def _shift_rows(z, prev, shift):
    zr = pltpu.roll(z, shift, 0)
    pr = pltpu.roll(prev, shift, 0)
    row = lax.broadcasted_iota(jnp.int32, prev.shape, 0)
    head = jnp.where(row < shift, pr, zr[:V7X_SUBLANES])
    return jnp.concatenate([head, zr[V7X_SUBLANES:]], axis=0)


def _mixer_in_kernel(x_ref, g_ref, w_ref, cw_ref, q_ref, k_ref, v_ref, cb_ref, carry_ref,
                     *, tiles_per_seq):
    i = pl.program_id(0)
    h = _rms_scale(x_ref[...], g_ref[...], NORM_EPS).astype(jnp.bfloat16)
    proj = jnp.dot(h, w_ref[...], preferred_element_type=jnp.float32)
    w = ATTN_W
    q_ref[...] = (proj[:, 0:w] * Q_SCALE).astype(jnp.bfloat16)
    k_ref[...] = proj[:, w:2 * w].astype(jnp.bfloat16)
    v_ref[...] = proj[:, 2 * w:3 * w].astype(jnp.bfloat16)
    c_b = proj[:, 3 * w:4 * w]
    z = proj[:, 4 * w:5 * w] * proj[:, 5 * w:6 * w]

    @pl.when(i % tiles_per_seq == 0)
    def _():
        carry_ref[...] = jnp.zeros_like(carry_ref)

    prev = carry_ref[...]
    cw = cw_ref[...]
    conv = cw[2:3] * z
    for tap in range(CONV_K - 1):
        conv = conv + cw[tap:tap + 1] * _shift_rows(z, prev, CONV_K - 1 - tap)
    cb_ref[...] = (c_b * conv).astype(jnp.bfloat16)
    carry_ref[...] = z[z.shape[0] - V7X_SUBLANES:]


def _mixer_in(x, g, w_in, conv_w, layer, seq_len):
    m, d = x.shape
    tm = MIX_TM
    n_in = 3 * ATTN_W + 3 * CONV_W
    est = (2 * tm * d * 4 + 2 * d * n_in * 2 + 2 * 4 * tm * ATTN_W * 2
           + tm * d * 2 + tm * n_in * 4 + 4 * tm * CONV_W * 4)
    out = jax.ShapeDtypeStruct((m, ATTN_W), jnp.bfloat16)
    row_spec = pl.BlockSpec((tm, ATTN_W), lambda i: (i, 0))
    return pl.pallas_call(
        functools.partial(_mixer_in_kernel, tiles_per_seq=seq_len // tm),
        grid=(m // tm,),
        in_specs=[
            pl.BlockSpec((tm, d), lambda i: (i, 0)),
            pl.BlockSpec((None, 1, d), lambda i: (layer, 0, 0)),
            pl.BlockSpec((None, d, n_in), lambda i: (layer, 0, 0)),
            pl.BlockSpec((None, CONV_K, CONV_W), lambda i: (layer, 0, 0)),
        ],
        out_specs=[row_spec, row_spec, row_spec, row_spec],
        out_shape=[out, out, out, out],
        scratch_shapes=[pltpu.VMEM((V7X_SUBLANES, CONV_W), jnp.float32)],
        compiler_params=pltpu.CompilerParams(
            dimension_semantics=("arbitrary",), vmem_limit_bytes=_vmem_limit(est)),
        name="mixer_in",
    )(x, g, w_in, conv_w)


def _bias_tile_kernel(tab_ref, o_ref):
    hd = pl.program_id(0)
    t = o_ref.shape[-1]
    row = lax.broadcasted_iota(jnp.int32, (t, t), 0)
    col = lax.broadcasted_iota(jnp.int32, (t, t), 1)
    far = tab_ref[NUM_BUCKETS - 1, hd] * LOG2E
    o_ref[0] = jnp.full((t, t), far, jnp.float32)
    for tile, offset in ((1, t), (2, 0)):
        dist = row - col + offset
        val = jnp.full((t, t), far, jnp.float32)
        for b in range(NUM_BUCKETS - 2, -1, -1):
            val = jnp.where(dist < BUCKET_UPPER[b], tab_ref[b, hd] * LOG2E, val)
        if offset == 0:
            val = jnp.where(dist < 0, MASK_VALUE, val)
        o_ref[tile] = val


def _bias_tiles(rel_bias):
    t = ATT_T
    assert t >= BIAS_CONST_FROM
    return pl.pallas_call(
        _bias_tile_kernel,
        grid=(N_HEADS,),
        in_specs=[pl.BlockSpec(memory_space=pltpu.SMEM)],
        out_specs=pl.BlockSpec((None, 3, t, t), lambda h: (h, 0, 0, 0)),
        out_shape=jax.ShapeDtypeStruct((N_HEADS, 3, t, t), jnp.float32),
        compiler_params=pltpu.CompilerParams(
            dimension_semantics=("arbitrary",), vmem_limit_bytes=_vmem_limit(16 * t * t * 4)),
        name="bias_tiles",
    )(rel_bias)


def _attn_kernel(q_ref, k_ref, v_ref, bias_ref, lam_ref, sg_ref, o_ref,
                 qq_ref, vext_ref, m_ref, acc_ref, *, lambda_init):
    i = pl.program_id(2)
    t = q_ref.shape[0]

    @pl.when(i == 0)
    def _():
        vext_ref[:, :V_DIM] = v_ref[...]
        vext_ref[:, V_DIM:] = jnp.ones((vext_ref.shape[0], V_DIM), jnp.bfloat16)

    q = q_ref[...]
    lane = lax.broadcasted_iota(jnp.int32, q.shape, 1)
    zero = jnp.zeros_like(q)
    qq_ref[:t] = jnp.where(lane < HEAD_DIM, q, zero)
    qq_ref[t:] = jnp.where(lane >= HEAD_DIM, q, zero)
    m_ref[...] = jnp.full(m_ref.shape, MASK_VALUE, jnp.float32)
    acc_ref[...] = jnp.zeros_like(acc_ref)

    def body(j, carry):
        start = pl.multiple_of(j * t, t)
        s = lax.dot_general(qq_ref[...], k_ref[pl.ds(start, t), :],
                            (((1,), (1,)), ((), ())),
                            preferred_element_type=jnp.float32)
        bias = bias_ref[jnp.clip(j - i + 2, 0, 2)]
        s = (s.reshape(2, t, t) + bias[None]).reshape(2 * t, t)
        m_prev = m_ref[...]
        m_new = jnp.maximum(m_prev, jnp.max(s, axis=-1, keepdims=True))
        p = jnp.exp2(s - m_new[:, :1])
        alpha = jnp.exp2(m_prev - m_new)
        pv = jnp.dot(p.astype(jnp.bfloat16), vext_ref[pl.ds(start, t), :],
                     preferred_element_type=jnp.float32)
        acc_ref[...] = acc_ref[...] * jnp.concatenate([alpha, alpha], axis=1) + pv
        m_ref[...] = m_new
        return carry

    lax.fori_loop(0, i + 1, body, 0)

    acc = acc_ref[...]
    o = acc[:, :V_DIM] / acc[:, V_DIM:]
    lv = lam_ref[...]
    lam = (jnp.exp(jnp.sum(lv[0:1] * lv[1:2])) - jnp.exp(jnp.sum(lv[2:3] * lv[3:4]))
           + lambda_init)
    diff = o[:t] - lam * o[t:]
    y = _rms_scale(diff, sg_ref[...], SUBLN_EPS) * (1.0 - lambda_init)
    o_ref[...] = y.astype(jnp.bfloat16)


def _attention(q, k, v, bias, lam_vecs, subln_g, layer, lambda_init):
    b, seq, _ = q.shape
    t = ATT_T
    est = (2 * 2 * t * V_DIM * 2 + 2 * 2 * seq * V_DIM * 2 + 2 * 3 * t * t * 4
           + 2 * t * V_DIM * 2 + seq * 2 * V_DIM * 2 + 2 * t * V_DIM * 4 + 2 * t * 2 * V_DIM * 4
           + 4 * 2 * t * t * 4)
    qo_spec = pl.BlockSpec((None, t, V_DIM), lambda h, bi, i: (bi, i, h))
    kv_spec = pl.BlockSpec((None, seq, V_DIM), lambda h, bi, i: (bi, 0, h))
    return pl.pallas_call(
        functools.partial(_attn_kernel, lambda_init=lambda_init),
        grid=(N_HEADS, b, seq // t),
        in_specs=[
            qo_spec, kv_spec, kv_spec,
            pl.BlockSpec((None, 3, t, t), lambda h, bi, i: (h, 0, 0, 0)),
            pl.BlockSpec((None, 4, HEAD_DIM), lambda h, bi, i: (layer, 0, 0)),
            pl.BlockSpec((None, 1, V_DIM), lambda h, bi, i: (layer, 0, 0)),
        ],
        out_specs=qo_spec,
        out_shape=jax.ShapeDtypeStruct(q.shape, jnp.bfloat16),
        scratch_shapes=[
            pltpu.VMEM((2 * t, V_DIM), jnp.bfloat16),
            pltpu.VMEM((seq, 2 * V_DIM), jnp.bfloat16),
            pltpu.VMEM((2 * t, V_DIM), jnp.float32),
            pltpu.VMEM((2 * t, 2 * V_DIM), jnp.float32),
        ],
        compiler_params=pltpu.CompilerParams(
            dimension_semantics=("arbitrary", "arbitrary", "arbitrary"),
            vmem_limit_bytes=_vmem_limit(est)),
        name="diff_attention",
    )(q, k, v, bias, lam_vecs, subln_g)


def _mixer_out_kernel(x_ref, g_ref, a_ref, cb_ref, wga_ref, wgc_ref, wba_ref, wbc_ref, wo_ref,
                      o_ref):
    x = x_ref[...]
    h = _rms_scale(x, g_ref[...], NORM_EPS).astype(jnp.bfloat16)
    g_attn = jnp.dot(h, wga_ref[...], preferred_element_type=jnp.float32)
    g_conv = jnp.dot(h, wgc_ref[...], preferred_element_type=jnp.float32)
    y_attn = jnp.dot(a_ref[...], wba_ref[...], preferred_element_type=jnp.float32)
    y_conv = jnp.dot(cb_ref[...], wbc_ref[...], preferred_element_type=jnp.float32)
    merged = jax.nn.sigmoid(g_attn) * y_attn + jax.nn.sigmoid(g_conv) * y_conv
    o_ref[...] = x + jnp.dot(merged.astype(jnp.bfloat16), wo_ref[...],
                             preferred_element_type=jnp.float32)


def _mixer_out(x, g, attn, convb, w_in, w_ba, w_bc, w_out, layer):
    m, d = x.shape
    tm = MIX_TM
    gate_block = (3 * ATTN_W + 3 * CONV_W) // d
    est = (4 * tm * d * 4 + 2 * 2 * tm * ATTN_W * 2 + 2 * (3 * d * d + 2 * ATTN_W * d) * 2
           + tm * d * 2 + 6 * tm * d * 4)
    return pl.pallas_call(
        _mixer_out_kernel,
        grid=(m // tm,),
        in_specs=[
            pl.BlockSpec((tm, d), lambda i: (i, 0)),
            pl.BlockSpec((None, 1, d), lambda i: (layer, 0, 0)),
            pl.BlockSpec((tm, ATTN_W), lambda i: (i, 0)),
            pl.BlockSpec((tm, CONV_W), lambda i: (i, 0)),
            pl.BlockSpec((None, d, d), lambda i: (layer, 0, gate_block)),
            pl.BlockSpec((None, d, d), lambda i: (layer, 0, gate_block + 1)),
            pl.BlockSpec((None, ATTN_W, d), lambda i: (layer, 0, 0)),
            pl.BlockSpec((None, CONV_W, d), lambda i: (layer, 0, 0)),
            pl.BlockSpec((None, d, d), lambda i: (layer, 0, 0)),
        ],
        out_specs=pl.BlockSpec((tm, d), lambda i: (i, 0)),
        out_shape=jax.ShapeDtypeStruct((m, d), jnp.float32),
        compiler_params=pltpu.CompilerParams(
            dimension_semantics=("arbitrary",), vmem_limit_bytes=_vmem_limit(est)),
        name="mixer_out",
    )(x, g, attn, convb, w_in, w_in, w_ba, w_bc, w_out)


def _final_norm_kernel(x_ref, g_ref, o_ref):
    o_ref[...] = _rms_scale(x_ref[...], g_ref[...], NORM_EPS)


def _final_norm(x, g):
    m, d = x.shape
    tm = MIX_TM
    return pl.pallas_call(
        _final_norm_kernel,
        grid=(m // tm,),
        in_specs=[pl.BlockSpec((tm, d), lambda i: (i, 0)), pl.BlockSpec((1, d), lambda i: (0, 0))],
        out_specs=pl.BlockSpec((tm, d), lambda i: (i, 0)),
        out_shape=jax.ShapeDtypeStruct((m, d), jnp.float32),
        compiler_params=pltpu.CompilerParams(
            dimension_semantics=("arbitrary",), vmem_limit_bytes=_vmem_limit(6 * tm * d * 4)),
        name="final_norm",
    )(x, g)


def kernel(x, norm_ffn1_g, ffn1_w_gate, ffn1_w_up, ffn1_w_down, norm_mix_g, w_in, lambda_q1, lambda_k1, lambda_q2, lambda_k2, subln_g, rel_bias, conv_w, w_branch_attn, w_branch_conv, w_out, norm_ffn2_g, ffn2_w_gate, ffn2_w_up, ffn2_w_down, final_g):
    batch, seq, d = x.shape
    depth = w_in.shape[0]
    bf = lambda a: a.astype(jnp.bfloat16)
    row3 = lambda a: a.reshape(a.shape[0], 1, a.shape[1])

    f1 = (row3(norm_ffn1_g), bf(ffn1_w_gate), bf(ffn1_w_up), bf(ffn1_w_down))
    f2 = (row3(norm_ffn2_g), bf(ffn2_w_gate), bf(ffn2_w_up), bf(ffn2_w_down))
    g_mix = row3(norm_mix_g)
    w_in_b, w_ba, w_bc, w_o = bf(w_in), bf(w_branch_attn), bf(w_branch_conv), bf(w_out)
    lam_vecs = jnp.stack([lambda_q1, lambda_k1, lambda_q2, lambda_k2], axis=1)
    sg = row3(subln_g)
    bias = _bias_tiles(rel_bias)

    xf = x.reshape(batch * seq, d)
    for layer in range(depth):
        lambda_init = 0.8 - 0.6 * math.exp(-0.3 * layer)
        xf = _ffn(xf, *f1, layer)
        q, k, v, convb = _mixer_in(xf, g_mix, w_in_b, conv_w, layer, seq)
        to3 = lambda a: a.reshape(batch, seq, ATTN_W)
        attn = _attention(to3(q), to3(k), to3(v), bias, lam_vecs, sg, layer, lambda_init)
        xf = _mixer_out(xf, g_mix, attn.reshape(batch * seq, ATTN_W), convb,
                        w_in_b, w_ba, w_bc, w_o, layer)
        xf = _ffn(xf, *f2, layer)
    return _final_norm(xf, final_g.reshape(1, d)).reshape(batch, seq, d)
```

```python
import functools
import math

import jax
import jax.numpy as jnp
from jax import lax
from jax.experimental import pallas as pl
from jax.experimental.pallas import tpu as pltpu

D_MODEL = 1024
HEAD_DIM = 64
V_DIM = 2 * HEAD_DIM
ATTN_W = D_MODEL // 2
N_HEADS = ATTN_W // V_DIM
CONV_W = D_MODEL // 2
CONV_K = 3
D_FF = (((8 * D_MODEL) // 3 + 255) // 256) * 256
NUM_BUCKETS = 32
MAX_EXACT = NUM_BUCKETS // 2
MAX_DISTANCE = 128
NORM_EPS = 1e-6
SUBLN_EPS = 1e-5

LOG2E = math.log2(math.e)
Q_SCALE = HEAD_DIM ** -0.5 * LOG2E
MASK_VALUE = -1e30

V7X_SUBLANES = 8
MIB = 1 << 20

FFN_TM = 512
FFN_TF = D_FF // 2
MIX_TM = 512
ATT_T = 512
ATT_SUB = 256


def _bucket_upper_bounds():
    def bucket(n):
        if n < MAX_EXACT:
            return n
        v = math.log(n / MAX_EXACT) / math.log(MAX_DISTANCE / MAX_EXACT)
        return min(MAX_EXACT + int(v * (NUM_BUCKETS - MAX_EXACT)), NUM_BUCKETS - 1)

    bounds = []
    n = 0
    for b in range(NUM_BUCKETS - 1):
        while bucket(n) <= b:
            n += 1
        bounds.append(n)
    return tuple(bounds)


BUCKET_UPPER = _bucket_upper_bounds()
BIAS_CONST_FROM = BUCKET_UPPER[-1]


def _rms_scale(x, g, eps):
    ms = jnp.mean(x * x, axis=-1, keepdims=True)
    return x * lax.rsqrt(ms + eps) * g


def _vmem_limit(nbytes):
    return int(-(-nbytes // MIB) * MIB)


def _ffn_kernel(x_ref, g_ref, wg_ref, wu_ref, wd_ref, o_ref, h_ref, acc_ref):
    j = pl.program_id(1)

    @pl.when(j == 0)
    def _():
        h_ref[...] = _rms_scale(x_ref[...], g_ref[...], NORM_EPS).astype(jnp.bfloat16)
        acc_ref[...] = jnp.zeros_like(acc_ref)

    h = h_ref[...]
    gate = jnp.dot(h, wg_ref[...], preferred_element_type=jnp.float32)
    up = jnp.dot(h, wu_ref[...], preferred_element_type=jnp.float32)
    act = (gate * jax.nn.sigmoid(gate) * up).astype(jnp.bfloat16)
    acc_ref[...] += jnp.dot(act, wd_ref[...], preferred_element_type=jnp.float32)

    @pl.when(j == pl.num_programs(1) - 1)
    def _():
        o_ref[...] = x_ref[...] + 0.5 * acc_ref[...]


def _ffn(x, g, wg, wu, wd, layer):
    m, d = x.shape
    f = wg.shape[-1]
    tm, tf = FFN_TM, FFN_TF
    est = (4 * tm * d * 4 + tm * d * 4 + tm * d * 2
           + 2 * 3 * d * tf * 2
           + 2 * tm * tf * 4 + tm * tf * 2 + tm * d * 4)
    return pl.pallas_call(
        _ffn_kernel,
        grid=(m // tm, f // tf),
        in_specs=[
            pl.BlockSpec((tm, d), lambda i, j: (i, 0)),
            pl.BlockSpec((None, 1, d), lambda i, j: (layer, 0, 0)),
            pl.BlockSpec((None, d, tf), lambda i, j: (layer, 0, j)),
            pl.BlockSpec((None, d, tf), lambda i, j: (layer, 0, j)),
            pl.BlockSpec((None, tf, d), lambda i, j: (layer, j, 0)),
        ],
        out_specs=pl.BlockSpec((tm, d), lambda i, j: (i, 0)),
        out_shape=jax.ShapeDtypeStruct((m, d), jnp.float32),
        scratch_shapes=[pltpu.VMEM((tm, d), jnp.bfloat16), pltpu.VMEM((tm, d), jnp.float32)],
        compiler_params=pltpu.CompilerParams(
            dimension_semantics=("arbitrary", "arbitrary"),
            vmem_limit_bytes=_vmem_limit(est)),
        name="ffn",
    )(x, g, wg, wu, wd)


def _shift_rows(z, prev, shift):
    zr = pltpu.roll(z, shift, 0)
    pr = pltpu.roll(prev, shift, 0)
    row = lax.broadcasted_iota(jnp.int32, prev.shape, 0)
    head = jnp.where(row < shift, pr, zr[:V7X_SUBLANES])
    return jnp.concatenate([head, zr[V7X_SUBLANES:]], axis=0)


def _mixer_in_kernel(x_ref, g_ref, w_ref, cw_ref, q_ref, k_ref, v_ref, cb_ref, carry_ref,
                     *, tiles_per_seq):
    i = pl.program_id(0)
    h = _rms_scale(x_ref[...], g_ref[...], NORM_EPS).astype(jnp.bfloat16)
    proj = jnp.dot(h, w_ref[...], preferred_element_type=jnp.float32)
    w = ATTN_W
    q_ref[...] = (proj[:, 0:w] * Q_SCALE).astype(jnp.bfloat16)
    k_ref[...] = proj[:, w:2 * w].astype(jnp.bfloat16)
    v_ref[...] = proj[:, 2 * w:3 * w].astype(jnp.bfloat16)
    c_b = proj[:, 3 * w:4 * w]
    z = proj[:, 4 * w:5 * w] * proj[:, 5 * w:6 * w]

    @pl.when(i % tiles_per_seq == 0)
    def _():
        carry_ref[...] = jnp.zeros_like(carry_ref)

    prev = carry_ref[...]
    cw = cw_ref[...]
    conv = cw[2:3] * z
    for tap in range(CONV_K - 1):
        conv = conv + cw[tap:tap + 1] * _shift_rows(z, prev, CONV_K - 1 - tap)
    cb_ref[...] = (c_b * conv).astype(jnp.bfloat16)
    carry_ref[...] = z[z.shape[0] - V7X_SUBLANES:]


def _mixer_in(x, g, w_in, conv_w, layer, seq_len):
    m, d = x.shape
    tm = MIX_TM
    n_in = 3 * ATTN_W + 3 * CONV_W
    est = (2 * tm * d * 4 + 2 * d * n_in * 2 + 2 * 4 * tm * ATTN_W * 2
           + tm * d * 2 + tm * n_in * 4 + 4 * tm * CONV_W * 4)
    out = jax.ShapeDtypeStruct((m, ATTN_W), jnp.bfloat16)
    row_spec = pl.BlockSpec((tm, ATTN_W), lambda i: (i, 0))
    return pl.pallas_call(
        functools.partial(_mixer_in_kernel, tiles_per_seq=seq_len // tm),
        grid=(m // tm,),
        in_specs=[
            pl.BlockSpec((tm, d), lambda i: (i, 0)),
            pl.BlockSpec((None, 1, d), lambda i: (layer, 0, 0)),
            pl.BlockSpec((None, d, n_in), lambda i: (layer, 0, 0)),
            pl.BlockSpec((None, CONV_K, CONV_W), lambda i: (layer, 0, 0)),
        ],
        out_specs=[row_spec, row_spec, row_spec, row_spec],
        out_shape=[out, out, out, out],
        scratch_shapes=[pltpu.VMEM((V7X_SUBLANES, CONV_W), jnp.float32)],
        compiler_params=pltpu.CompilerParams(
            dimension_semantics=("arbitrary",), vmem_limit_bytes=_vmem_limit(est)),
        name="mixer_in",
    )(x, g, w_in, conv_w)


def _bias_tile_kernel(tab_ref, o_ref):
    hd = pl.program_id(0)
    t = o_ref.shape[-1]
    row = lax.broadcasted_iota(jnp.int32, (t, t), 0)
    col = lax.broadcasted_iota(jnp.int32, (t, t), 1)
    far = tab_ref[NUM_BUCKETS - 1, hd] * LOG2E
    o_ref[0] = jnp.full((t, t), far, jnp.float32)
    for tile, offset in ((1, t), (2, 0)):
        dist = row - col + offset
        val = jnp.full((t, t), far, jnp.float32)
        for b in range(NUM_BUCKETS - 2, -1, -1):
            val = jnp.where(dist < BUCKET_UPPER[b], tab_ref[b, hd] * LOG2E, val)
        if offset == 0:
            val = jnp.where(dist < 0, MASK_VALUE, val)
        o_ref[tile] = val


def _bias_tiles(rel_bias):
    t = ATT_T
    assert t >= BIAS_CONST_FROM
    return pl.pallas_call(
        _bias_tile_kernel,
        grid=(N_HEADS,),
        in_specs=[pl.BlockSpec(memory_space=pltpu.SMEM)],
        out_specs=pl.BlockSpec((None, 3, t, t), lambda h: (h, 0, 0, 0)),
        out_shape=jax.ShapeDtypeStruct((N_HEADS, 3, t, t), jnp.float32),
        compiler_params=pltpu.CompilerParams(
            dimension_semantics=("arbitrary",), vmem_limit_bytes=_vmem_limit(16 * t * t * 4)),
        name="bias_tiles",
    )(rel_bias)


def _attn_kernel(q_ref, k_ref, v_ref, bias_ref, lam_ref, sg_ref, o_ref,
                 qq_ref, vext_ref, m_ref, acc_ref, s_ref, mblk_ref, p_ref, alpha_ref,
                 *, lambda_init):
    t = ATT_T
    n_q = q_ref.shape[0] // t
    n_pairs = n_q * (n_q + 1) // 2
    n_sub = 2 * t // ATT_SUB

    vext_ref[:, :V_DIM] = v_ref[...]
    vext_ref[:, V_DIM:] = jnp.ones((vext_ref.shape[0], V_DIM), jnp.bfloat16)
    for qi in range(n_q):
        q = q_ref[pl.ds(qi * t, t), :]
        lane = lax.broadcasted_iota(jnp.int32, q.shape, 1)
        zero = jnp.zeros_like(q)
        qq_ref[qi, :t, :] = jnp.where(lane < HEAD_DIM, q, zero)
        qq_ref[qi, t:, :] = jnp.where(lane >= HEAD_DIM, q, zero)
    m_ref[...] = jnp.full(m_ref.shape, MASK_VALUE, jnp.float32)
    acc_ref[...] = jnp.zeros_like(acc_ref)
    p_ref[...] = jnp.zeros_like(p_ref)
    alpha_ref[...] = jnp.ones_like(alpha_ref)

    def score_stage(i, j):
        kblk = k_ref[pl.ds(pl.multiple_of(j * t, t), t), :]
        tile = jnp.clip(j - i + 2, 0, 2)
        for r in range(n_sub):
            rows = pl.ds(r * ATT_SUB, ATT_SUB)
            s = lax.dot_general(qq_ref[i, rows, :], kblk, (((1,), (1,)), ((), ())),
                                preferred_element_type=jnp.float32)
            s = s + bias_ref[tile, pl.ds((r * ATT_SUB) % t, ATT_SUB), :]
            s_ref[rows, :] = s
            mblk_ref[rows, :] = jnp.broadcast_to(jnp.max(s, axis=-1, keepdims=True),
                                                 (ATT_SUB, V_DIM))

    def softmax_stage(i):
        for r in range(n_sub):
            rows = pl.ds(r * ATT_SUB, ATT_SUB)
            m_prev = m_ref[i, rows, :]
            m_new = jnp.maximum(m_prev, mblk_ref[rows, :])
            p_ref[rows, :] = jnp.exp2(s_ref[rows, :] - m_new[:, :1]).astype(jnp.bfloat16)
            alpha_ref[rows, :] = jnp.exp2(m_prev - m_new)
            m_ref[i, rows, :] = m_new

    def value_stage(i, j):
        vblk = vext_ref[pl.ds(pl.multiple_of(j * t, t), t), :]
        for r in range(n_sub):
            rows = pl.ds(r * ATT_SUB, ATT_SUB)
            pv = jnp.dot(p_ref[rows, :], vblk, preferred_element_type=jnp.float32)
            alpha = alpha_ref[rows, :]
            acc_ref[i, rows, :] = (acc_ref[i, rows, :] * jnp.concatenate([alpha, alpha], axis=1)
                                   + pv)

    def next_pair(i, j):
        wrap = j == i
        return jnp.where(wrap, i + 1, i), jnp.where(wrap, 0, j + 1)

    zero_i = jnp.int32(0)
    score_stage(zero_i, zero_i)

    def body(n, carry):
        (iv, jv), (ic, jc) = carry
        value_stage(iv, jv)
        softmax_stage(ic)
        inx, jnx = next_pair(ic, jc)
        keep = n == n_pairs - 1
        score_stage(jnp.where(keep, ic, inx), jnp.where(keep, jc, jnx))
        return (ic, jc), (inx, jnx)

    (iv, jv), _ = lax.fori_loop(0, n_pairs, body, ((zero_i, zero_i), (zero_i, zero_i)))
    value_stage(iv, jv)

    lv = lam_ref[...]
    lam = (jnp.exp(jnp.sum(lv[0:1] * lv[1:2])) - jnp.exp(jnp.sum(lv[2:3] * lv[3:4]))
           + lambda_init)
    for qi in range(n_q):
        acc = acc_ref[qi]
        o = acc[:, :V_DIM] / acc[:, V_DIM:]
        diff = o[:t] - lam * o[t:]
        y = _rms_scale(diff, sg_ref[...], SUBLN_EPS) * (1.0 - lambda_init)
        o_ref[pl.ds(qi * t, t), :] = y.astype(jnp.bfloat16)


def _attention(q, k, v, bias, lam_vecs, subln_g, layer, lambda_init):
    b, seq, _ = q.shape
    t = ATT_T
    n_q = seq // t
    est = (2 * 4 * seq * V_DIM * 2 + 2 * 3 * t * t * 4
           + n_q * 2 * t * V_DIM * 2 + seq * 2 * V_DIM * 2
           + n_q * 2 * t * V_DIM * 4 + n_q * 2 * t * 2 * V_DIM * 4
           + 2 * t * t * 4 + 2 * t * t * 2 + 2 * 2 * t * V_DIM * 4
           + 4 * ATT_SUB * t * 4)
    seq_spec = pl.BlockSpec((None, seq, V_DIM), lambda h, bi: (bi, 0, h))
    return pl.pallas_call(
        functools.partial(_attn_kernel, lambda_init=lambda_init),
        grid=(N_HEADS, b),
        in_specs=[
            seq_spec, seq_spec, seq_spec,
            pl.BlockSpec((None, 3, t, t), lambda h, bi: (h, 0, 0, 0)),
            pl.BlockSpec((None, 4, HEAD_DIM), lambda h, bi: (layer, 0, 0)),
            pl.BlockSpec((None, 1, V_DIM), lambda h, bi: (layer, 0, 0)),
        ],
        out_specs=seq_spec,
        out_shape=jax.ShapeDtypeStruct(q.shape, jnp.bfloat16),
        scratch_shapes=[
            pltpu.VMEM((n_q, 2 * t, V_DIM), jnp.bfloat16),
            pltpu.VMEM((seq, 2 * V_DIM), jnp.bfloat16),
            pltpu.VMEM((n_q, 2 * t, V_DIM), jnp.float32),
            pltpu.VMEM((n_q, 2 * t, 2 * V_DIM), jnp.float32),
            pltpu.VMEM((2 * t, t), jnp.float32),
            pltpu.VMEM((2 * t, V_DIM), jnp.float32),
            pltpu.VMEM((2 * t, t), jnp.bfloat16),
            pltpu.VMEM((2 * t, V_DIM), jnp.float32),
        ],
        compiler_params=pltpu.CompilerParams(
            dimension_semantics=("arbitrary", "arbitrary"),
            vmem_limit_bytes=_vmem_limit(est)),
        name="diff_attention",
    )(q, k, v, bias, lam_vecs, subln_g)


def _mixer_out_kernel(x_ref, g_ref, a_ref, cb_ref, wga_ref, wgc_ref, wba_ref, wbc_ref, wo_ref,
                      o_ref):
    x = x_ref[...]
    h = _rms_scale(x, g_ref[...], NORM_EPS).astype(jnp.bfloat16)
    g_attn = jnp.dot(h, wga_ref[...], preferred_element_type=jnp.float32)
    g_conv = jnp.dot(h, wgc_ref[...], preferred_element_type=jnp.float32)
    y_attn = jnp.dot(a_ref[...], wba_ref[...], preferred_element_type=jnp.float32)
    y_conv = jnp.dot(cb_ref[...], wbc_ref[...], preferred_element_type=jnp.float32)
    merged = jax.nn.sigmoid(g_attn) * y_attn + jax.nn.sigmoid(g_conv) * y_conv
    o_ref[...] = x + jnp.dot(merged.astype(jnp.bfloat16), wo_ref[...],
                             preferred_element_type=jnp.float32)


def _mixer_out(x, g, attn, convb, w_in, w_ba, w_bc, w_out, layer):
    m, d = x.shape
    tm = MIX_TM
    gate_block = (3 * ATTN_W + 3 * CONV_W) // d
    est = (4 * tm * d * 4 + 2 * 2 * tm * ATTN_W * 2 + 2 * (3 * d * d + 2 * ATTN_W * d) * 2
           + tm * d * 2 + 6 * tm * d * 4)
    return pl.pallas_call(
        _mixer_out_kernel,
        grid=(m // tm,),
        in_specs=[
            pl.BlockSpec((tm, d), lambda i: (i, 0)),
            pl.BlockSpec((None, 1, d), lambda i: (layer, 0, 0)),
            pl.BlockSpec((tm, ATTN_W), lambda i: (i, 0)),
            pl.BlockSpec((tm, CONV_W), lambda i: (i, 0)),
            pl.BlockSpec((None, d, d), lambda i: (layer, 0, gate_block)),
            pl.BlockSpec((None, d, d), lambda i: (layer, 0, gate_block + 1)),
            pl.BlockSpec((None, ATTN_W, d), lambda i: (layer, 0, 0)),
            pl.BlockSpec((None, CONV_W, d), lambda i: (layer, 0, 0)),
            pl.BlockSpec((None, d, d), lambda i: (layer, 0, 0)),
        ],
        out_specs=pl.BlockSpec((tm, d), lambda i: (i, 0)),
        out_shape=jax.ShapeDtypeStruct((m, d), jnp.float32),
        compiler_params=pltpu.CompilerParams(
            dimension_semantics=("arbitrary",), vmem_limit_bytes=_vmem_limit(est)),
        name="mixer_out",
    )(x, g, attn, convb, w_in, w_in, w_ba, w_bc, w_out)


def _final_norm_kernel(x_ref, g_ref, o_ref):
    o_ref[...] = _rms_scale(x_ref[...], g_ref[...], NORM_EPS)


def _final_norm(x, g):
    m, d = x.shape
    tm = MIX_TM
    return pl.pallas_call(
        _final_norm_kernel,
        grid=(m // tm,),
        in_specs=[pl.BlockSpec((tm, d), lambda i: (i, 0)), pl.BlockSpec((1, d), lambda i: (0, 0))],
        out_specs=pl.BlockSpec((tm, d), lambda i: (i, 0)),
        out_shape=jax.ShapeDtypeStruct((m, d), jnp.float32),
        compiler_params=pltpu.CompilerParams(
            dimension_semantics=("arbitrary",), vmem_limit_bytes=_vmem_limit(6 * tm * d * 4)),
        name="final_norm",
    )(x, g)


def kernel(x, norm_ffn1_g, ffn1_w_gate, ffn1_w_up, ffn1_w_down, norm_mix_g, w_in, lambda_q1, lambda_k1, lambda_q2, lambda_k2, subln_g, rel_bias, conv_w, w_branch_attn, w_branch_conv, w_out, norm_ffn2_g, ffn2_w_gate, ffn2_w_up, ffn2_w_down, final_g):
    batch, seq, d = x.shape
    depth = w_in.shape[0]
    bf = lambda a: a.astype(jnp.bfloat16)
    row3 = lambda a: a.reshape(a.shape[0], 1, a.shape[1])

    f1 = (row3(norm_ffn1_g), bf(ffn1_w_gate), bf(ffn1_w_up), bf(ffn1_w_down))
    f2 = (row3(norm_ffn2_g), bf(ffn2_w_gate), bf(ffn2_w_up), bf(ffn2_w_down))
    g_mix = row3(norm_mix_g)
    w_in_b, w_ba, w_bc, w_o = bf(w_in), bf(w_branch_attn), bf(w_branch_conv), bf(w_out)
    lam_vecs = jnp.stack([lambda_q1, lambda_k1, lambda_q2, lambda_k2], axis=1)
    sg = row3(subln_g)
    bias = _bias_tiles(rel_bias)

    xf = x.reshape(batch * seq, d)
    for layer in range(depth):
        lambda_init = 0.8 - 0.6 * math.exp(-0.3 * layer)
        xf = _ffn(xf, *f1, layer)
        q, k, v, convb = _mixer_in(xf, g_mix, w_in_b, conv_w, layer, seq)
        to3 = lambda a: a.reshape(batch, seq, ATTN_W)
        attn = _attention(to3(q), to3(k), to3(v), bias, lam_vecs, sg, layer, lambda_init)
        xf = _mixer_out(xf, g_mix, attn.reshape(batch * seq, ATTN_W), convb,
                        w_in_b, w_ba, w_bc, w_o, layer)
        xf = _ffn(xf, *f2, layer)
    return _final_norm(xf, final_g.reshape(1, d)).reshape(batch, seq, d)
```

```python
import functools
import math

import jax
import jax.numpy as jnp
from jax import lax
from jax.experimental import pallas as pl
from jax.experimental.pallas import tpu as pltpu

D_MODEL = 1024
HEAD_DIM = 64
V_DIM = 2 * HEAD_DIM
ATTN_W = D_MODEL // 2
N_HEADS = ATTN_W // V_DIM
CONV_W = D_MODEL // 2
CONV_K = 3
D_FF = (((8 * D_MODEL) // 3 + 255) // 256) * 256
NUM_BUCKETS = 32
MAX_EXACT = NUM_BUCKETS // 2
MAX_DISTANCE = 128
NORM_EPS = 1e-6
SUBLN_EPS = 1e-5

LOG2E = math.log2(math.e)
Q_SCALE = HEAD_DIM ** -0.5 * LOG2E
MASK_VALUE = -1e30

V7X_SUBLANES = 8
MIB = 1 << 20

ROW_TM = 512
ROW_SUB = 256
ATT_T = 512
ATT_SUB = 256


def _bucket_upper_bounds():
    def bucket(n):
        if n < MAX_EXACT:
            return n
        v = math.log(n / MAX_EXACT) / math.log(MAX_DISTANCE / MAX_EXACT)
        return min(MAX_EXACT + int(v * (NUM_BUCKETS - MAX_EXACT)), NUM_BUCKETS - 1)

    bounds = []
    n = 0
    for b in range(NUM_BUCKETS - 1):
        while bucket(n) <= b:
            n += 1
        bounds.append(n)
    return tuple(bounds)


BUCKET_UPPER = _bucket_upper_bounds()
BIAS_CONST_FROM = BUCKET_UPPER[-1]


def _rms_scale(x, g, eps):
    ms = jnp.mean(x * x, axis=-1, keepdims=True)
    return x * lax.rsqrt(ms + eps) * g


def _vmem_limit(nbytes):
    return int(-(-nbytes // MIB) * MIB)


def _ffn_rows(x, g_ref, wg_ref, wu_ref, wd_ref):
    h = _rms_scale(x, g_ref[...], NORM_EPS).astype(jnp.bfloat16)
    gate = jnp.dot(h, wg_ref[...], preferred_element_type=jnp.float32)
    up = jnp.dot(h, wu_ref[...], preferred_element_type=jnp.float32)
    act = (gate * jax.nn.sigmoid(gate) * up).astype(jnp.bfloat16)
    return x + 0.5 * jnp.dot(act, wd_ref[...], preferred_element_type=jnp.float32)


def _resident(block_shape, index_map):
    return pl.BlockSpec(block_shape, index_map, pipeline_mode=pl.Buffered(1))


def _ffn_specs(layer, d, f):
    return [
        _resident((None, 1, d), lambda i: (layer, 0, 0)),
        _resident((None, d, f), lambda i: (layer, 0, 0)),
        _resident((None, d, f), lambda i: (layer, 0, 0)),
        _resident((None, f, d), lambda i: (layer, 0, 0)),
    ]


def _ffn_vmem_bytes(d, f):
    weights = 3 * d * f * 2
    temporaries = 2 * ROW_SUB * f * 4 + ROW_SUB * f * 2 + 2 * ROW_SUB * d * 4
    return weights + temporaries


def _shift_rows(z, prev, shift):
    zr = pltpu.roll(z, shift, 0)
    pr = pltpu.roll(prev, shift, 0)
    row = lax.broadcasted_iota(jnp.int32, prev.shape, 0)
    head = jnp.where(row < shift, pr, zr[:V7X_SUBLANES])
    return jnp.concatenate([head, zr[V7X_SUBLANES:]], axis=0)


def _layer_in_kernel(x_ref, g1_ref, wg_ref, wu_ref, wd_ref, gm_ref, w_ref, cw_ref,
                     x1_ref, q_ref, k_ref, v_ref, cb_ref, carry_ref, *, tiles_per_seq):
    starts_sequence = pl.program_id(0) % tiles_per_seq == 0
    prev = jnp.where(starts_sequence, 0.0, carry_ref[...])
    cw = cw_ref[...]
    w = ATTN_W
    for r in range(x_ref.shape[0] // ROW_SUB):
        rows = pl.ds(r * ROW_SUB, ROW_SUB)
        x1 = _ffn_rows(x_ref[rows, :], g1_ref, wg_ref, wu_ref, wd_ref)
        x1_ref[rows, :] = x1
        h = _rms_scale(x1, gm_ref[...], NORM_EPS).astype(jnp.bfloat16)
        proj = jnp.dot(h, w_ref[...], preferred_element_type=jnp.float32)
        q_ref[rows, :] = (proj[:, 0:w] * Q_SCALE).astype(jnp.bfloat16)
        k_ref[rows, :] = proj[:, w:2 * w].astype(jnp.bfloat16)
        v_ref[rows, :] = proj[:, 2 * w:3 * w].astype(jnp.bfloat16)
        c_b = proj[:, 3 * w:4 * w]
        z = proj[:, 4 * w:5 * w] * proj[:, 5 * w:6 * w]
        conv = cw[CONV_K - 1:CONV_K] * z
        for tap in range(CONV_K - 1):
            conv = conv + cw[tap:tap + 1] * _shift_rows(z, prev, CONV_K - 1 - tap)
        cb_ref[rows, :] = (c_b * conv).astype(jnp.bfloat16)
        prev = z[ROW_SUB - V7X_SUBLANES:]
    carry_ref[...] = prev


def _layer_in(x, ffn, g_mix, w_in, conv_w, layer, seq_len):
    m, d = x.shape
    f = ffn[1].shape[-1]
    tm = ROW_TM
    n_in = 3 * ATTN_W + 3 * CONV_W
    est = (_ffn_vmem_bytes(d, f) + d * n_in * 2 + 4 * tm * d * 4 + 2 * 4 * tm * ATTN_W * 2
           + 2 * (ROW_SUB * n_in * 4 + 4 * ROW_SUB * CONV_W * 4))
    out = jax.ShapeDtypeStruct((m, ATTN_W), jnp.bfloat16)
    half_spec = pl.BlockSpec((tm, ATTN_W), lambda i: (i, 0))
    row_spec = pl.BlockSpec((tm, d), lambda i: (i, 0))
    return pl.pallas_call(
        functools.partial(_layer_in_kernel, tiles_per_seq=seq_len // tm),
        grid=(m // tm,),
        in_specs=[row_spec] + _ffn_specs(layer, d, f) + [
            _resident((None, 1, d), lambda i: (layer, 0, 0)),
            _resident((None, d, n_in), lambda i: (layer, 0, 0)),
            _resident((None, CONV_K, CONV_W), lambda i: (layer, 0, 0)),
        ],
        out_specs=[row_spec, half_spec, half_spec, half_spec, half_spec],
        out_shape=[jax.ShapeDtypeStruct((m, d), jnp.float32), out, out, out, out],
        scratch_shapes=[pltpu.VMEM((V7X_SUBLANES, CONV_W), jnp.float32)],
        compiler_params=pltpu.CompilerParams(
            dimension_semantics=("arbitrary",), vmem_limit_bytes=_vmem_limit(est)),
        name="layer_in",
    )(x, *ffn, g_mix, w_in, conv_w)


def _bias_tile_kernel(tab_ref, o_ref):
    hd = pl.program_id(0)
    t = o_ref.shape[-1]
    row = lax.broadcasted_iota(jnp.int32, (t, t), 0)
    col = lax.broadcasted_iota(jnp.int32, (t, t), 1)
    far = tab_ref[NUM_BUCKETS - 1, hd] * LOG2E
    o_ref[0] = jnp.full((t, t), far, jnp.float32)
    for tile, offset in ((1, t), (2, 0)):
        dist = row - col + offset
        val = jnp.full((t, t), far, jnp.float32)
        for b in range(NUM_BUCKETS - 2, -1, -1):
            val = jnp.where(dist < BUCKET_UPPER[b], tab_ref[b, hd] * LOG2E, val)
        if offset == 0:
            val = jnp.where(dist < 0, MASK_VALUE, val)
        o_ref[tile] = val


def _bias_tiles(rel_bias):
    t = ATT_T
    assert t >= BIAS_CONST_FROM
    return pl.pallas_call(
        _bias_tile_kernel,
        grid=(N_HEADS,),
        in_specs=[pl.BlockSpec(memory_space=pltpu.SMEM)],
        out_specs=pl.BlockSpec((None, 3, t, t), lambda h: (h, 0, 0, 0)),
        out_shape=jax.ShapeDtypeStruct((N_HEADS, 3, t, t), jnp.float32),
        compiler_params=pltpu.CompilerParams(
            dimension_semantics=("arbitrary",), vmem_limit_bytes=_vmem_limit(16 * t * t * 4)),
        name="bias_tiles",
    )(rel_bias)


def _attn_kernel(q_ref, k_ref, v_ref, bias_ref, lam_ref, sg_ref, o_ref,
                 qq_ref, vext_ref, m_ref, acc_ref, s_ref, mblk_ref, p_ref, alpha_ref,
                 *, lambda_init):
    t = ATT_T
    n_q = q_ref.shape[0] // t
    n_pairs = n_q * (n_q + 1) // 2
    n_sub = 2 * t // ATT_SUB

    vext_ref[:, :V_DIM] = v_ref[...]
    vext_ref[:, V_DIM:] = jnp.ones((vext_ref.shape[0], V_DIM), jnp.bfloat16)
    for qi in range(n_q):
        q = q_ref[pl.ds(qi * t, t), :]
        lane = lax.broadcasted_iota(jnp.int32, q.shape, 1)
        zero = jnp.zeros_like(q)
        qq_ref[qi, :t, :] = jnp.where(lane < HEAD_DIM, q, zero)
        qq_ref[qi, t:, :] = jnp.where(lane >= HEAD_DIM, q, zero)
    m_ref[...] = jnp.full(m_ref.shape, MASK_VALUE, jnp.float32)
    acc_ref[...] = jnp.zeros_like(acc_ref)
    p_ref[...] = jnp.zeros_like(p_ref)
    alpha_ref[...] = jnp.ones_like(alpha_ref)

    def score_stage(i, j):
        kblk = k_ref[pl.ds(pl.multiple_of(j * t, t), t), :]
        tile = jnp.clip(j - i + 2, 0, 2)
        for r in range(n_sub):
            rows = pl.ds(r * ATT_SUB, ATT_SUB)
            s = lax.dot_general(qq_ref[i, rows, :], kblk, (((1,), (1,)), ((), ())),
                                preferred_element_type=jnp.float32)
            s = s + bias_ref[tile, pl.ds((r * ATT_SUB) % t, ATT_SUB), :]
            s_ref[rows, :] = s
            mblk_ref[rows, :] = jnp.broadcast_to(jnp.max(s, axis=-1, keepdims=True),
                                                 (ATT_SUB, V_DIM))

    def softmax_stage(i):
        for r in range(n_sub):
            rows = pl.ds(r * ATT_SUB, ATT_SUB)
            m_prev = m_ref[i, rows, :]
            m_new = jnp.maximum(m_prev, mblk_ref[rows, :])
            p_ref[rows, :] = jnp.exp2(s_ref[rows, :] - m_new[:, :1]).astype(jnp.bfloat16)
            alpha_ref[rows, :] = jnp.exp2(m_prev - m_new)
            m_ref[i, rows, :] = m_new

    def value_stage(i, j):
        vblk = vext_ref[pl.ds(pl.multiple_of(j * t, t), t), :]
        for r in range(n_sub):
            rows = pl.ds(r * ATT_SUB, ATT_SUB)
            pv = jnp.dot(p_ref[rows, :], vblk, preferred_element_type=jnp.float32)
            alpha = alpha_ref[rows, :]
            acc_ref[i, rows, :] = (acc_ref[i, rows, :] * jnp.concatenate([alpha, alpha], axis=1)
                                   + pv)

    def next_pair(i, j):
        wrap = j == i
        return jnp.where(wrap, i + 1, i), jnp.where(wrap, 0, j + 1)

    zero_i = jnp.int32(0)
    score_stage(zero_i, zero_i)

    def body(n, carry):
        (iv, jv), (ic, jc) = carry
        value_stage(iv, jv)
        softmax_stage(ic)
        inx, jnx = next_pair(ic, jc)
        keep = n == n_pairs - 1
        score_stage(jnp.where(keep, ic, inx), jnp.where(keep, jc, jnx))
        return (ic, jc), (inx, jnx)

    (iv, jv), _ = lax.fori_loop(0, n_pairs, body, ((zero_i, zero_i), (zero_i, zero_i)))
    value_stage(iv, jv)

    lv = lam_ref[...]
    lam = (jnp.exp(jnp.sum(lv[0:1] * lv[1:2])) - jnp.exp(jnp.sum(lv[2:3] * lv[3:4]))
           + lambda_init)
    for qi in range(n_q):
        acc = acc_ref[qi]
        o = acc[:, :V_DIM] / acc[:, V_DIM:]
        diff = o[:t] - lam * o[t:]
        y = _rms_scale(diff, sg_ref[...], SUBLN_EPS) * (1.0 - lambda_init)
        o_ref[pl.ds(qi * t, t), :] = y.astype(jnp.bfloat16)


def _attention(q, k, v, bias, lam_vecs, subln_g, layer, lambda_init):
    b, seq, _ = q.shape
    t = ATT_T
    n_q = seq // t
    est = (2 * 4 * seq * V_DIM * 2 + 2 * 3 * t * t * 4
           + n_q * 2 * t * V_DIM * 2 + seq * 2 * V_DIM * 2
           + n_q * 2 * t * V_DIM * 4 + n_q * 2 * t * 2 * V_DIM * 4
           + 2 * t * t * 4 + 2 * t * t * 2 + 2 * 2 * t * V_DIM * 4
           + 4 * ATT_SUB * t * 4)
    seq_spec = pl.BlockSpec((None, seq, V_DIM), lambda h, bi: (bi, 0, h))
    return pl.pallas_call(
        functools.partial(_attn_kernel, lambda_init=lambda_init),
        grid=(N_HEADS, b),
        in_specs=[
            seq_spec, seq_spec, seq_spec,
            pl.BlockSpec((None, 3, t, t), lambda h, bi: (h, 0, 0, 0)),
            pl.BlockSpec((None, 4, HEAD_DIM), lambda h, bi: (layer, 0, 0)),
            pl.BlockSpec((None, 1, V_DIM), lambda h, bi: (layer, 0, 0)),
        ],
        out_specs=seq_spec,
        out_shape=jax.ShapeDtypeStruct(q.shape, jnp.bfloat16),
        scratch_shapes=[
            pltpu.VMEM((n_q, 2 * t, V_DIM), jnp.bfloat16),
            pltpu.VMEM((seq, 2 * V_DIM), jnp.bfloat16),
            pltpu.VMEM((n_q, 2 * t, V_DIM), jnp.float32),
            pltpu.VMEM((n_q, 2 * t, 2 * V_DIM), jnp.float32),
            pltpu.VMEM((2 * t, t), jnp.float32),
            pltpu.VMEM((2 * t, V_DIM), jnp.float32),
            pltpu.VMEM((2 * t, t), jnp.bfloat16),
            pltpu.VMEM((2 * t, V_DIM), jnp.float32),
        ],
        compiler_params=pltpu.CompilerParams(
            dimension_semantics=("arbitrary", "arbitrary"),
            vmem_limit_bytes=_vmem_limit(est)),
        name="diff_attention",
    )(q, k, v, bias, lam_vecs, subln_g)


def _layer_out_kernel(x_ref, gm_ref, a_ref, cb_ref, wga_ref, wgc_ref, wba_ref, wbc_ref, wo_ref,
                      g2_ref, wg_ref, wu_ref, wd_ref, o_ref):
    for r in range(x_ref.shape[0] // ROW_SUB):
        rows = pl.ds(r * ROW_SUB, ROW_SUB)
        x = x_ref[rows, :]
        h = _rms_scale(x, gm_ref[...], NORM_EPS).astype(jnp.bfloat16)
        g_attn = jnp.dot(h, wga_ref[...], preferred_element_type=jnp.float32)
        g_conv = jnp.dot(h, wgc_ref[...], preferred_element_type=jnp.float32)
        y_attn = jnp.dot(a_ref[rows, :], wba_ref[...], preferred_element_type=jnp.float32)
        y_conv = jnp.dot(cb_ref[rows, :], wbc_ref[...], preferred_element_type=jnp.float32)
        merged = jax.nn.sigmoid(g_attn) * y_attn + jax.nn.sigmoid(g_conv) * y_conv
        x2 = x + jnp.dot(merged.astype(jnp.bfloat16), wo_ref[...],
                         preferred_element_type=jnp.float32)
        o_ref[rows, :] = _ffn_rows(x2, g2_ref, wg_ref, wu_ref, wd_ref)


def _layer_out(x, g_mix, attn, convb, w_in, w_ba, w_bc, w_out, ffn, layer):
    m, d = x.shape
    f = ffn[1].shape[-1]
    tm = ROW_TM
    gate_block = (3 * ATTN_W + 3 * CONV_W) // d
    est = (_ffn_vmem_bytes(d, f) + (3 * d * d + 2 * ATTN_W * d) * 2 + 4 * tm * d * 4
           + 2 * 2 * tm * ATTN_W * 2 + 2 * 6 * ROW_SUB * d * 4)
    row_spec = pl.BlockSpec((tm, d), lambda i: (i, 0))
    half_spec = pl.BlockSpec((tm, ATTN_W), lambda i: (i, 0))
    return pl.pallas_call(
        _layer_out_kernel,
        grid=(m // tm,),
        in_specs=[
            row_spec,
            _resident((None, 1, d), lambda i: (layer, 0, 0)),
            half_spec, half_spec,
            _resident((None, d, d), lambda i: (layer, 0, gate_block)),
            _resident((None, d, d), lambda i: (layer, 0, gate_block + 1)),
            _resident((None, ATTN_W, d), lambda i: (layer, 0, 0)),
            _resident((None, CONV_W, d), lambda i: (layer, 0, 0)),
            _resident((None, d, d), lambda i: (layer, 0, 0)),
        ] + _ffn_specs(layer, d, f),
        out_specs=row_spec,
        out_shape=jax.ShapeDtypeStruct((m, d), jnp.float32),
        compiler_params=pltpu.CompilerParams(
            dimension_semantics=("arbitrary",), vmem_limit_bytes=_vmem_limit(est)),
        name="layer_out",
    )(x, g_mix, attn, convb, w_in, w_in, w_ba, w_bc, w_out, *ffn)


def _final_norm_kernel(x_ref, g_ref, o_ref):
    o_ref[...] = _rms_scale(x_ref[...], g_ref[...], NORM_EPS)


def _final_norm(x, g):
    m, d = x.shape
    tm = ROW_TM
    return pl.pallas_call(
        _final_norm_kernel,
        grid=(m // tm,),
        in_specs=[pl.BlockSpec((tm, d), lambda i: (i, 0)), pl.BlockSpec((1, d), lambda i: (0, 0))],
        out_specs=pl.BlockSpec((tm, d), lambda i: (i, 0)),
        out_shape=jax.ShapeDtypeStruct((m, d), jnp.float32),
        compiler_params=pltpu.CompilerParams(
            dimension_semantics=("arbitrary",), vmem_limit_bytes=_vmem_limit(6 * tm * d * 4)),
        name="final_norm",
    )(x, g)


def kernel(x, norm_ffn1_g, ffn1_w_gate, ffn1_w_up, ffn1_w_down, norm_mix_g, w_in, lambda_q1, lambda_k1, lambda_q2, lambda_k2, subln_g, rel_bias, conv_w, w_branch_attn, w_branch_conv, w_out, norm_ffn2_g, ffn2_w_gate, ffn2_w_up, ffn2_w_down, final_g):
    batch, seq, d = x.shape
    depth = w_in.shape[0]
    bf = lambda a: a.astype(jnp.bfloat16)
    row3 = lambda a: a.reshape(a.shape[0], 1, a.shape[1])

    f1 = (row3(norm_ffn1_g), bf(ffn1_w_gate), bf(ffn1_w_up), bf(ffn1_w_down))
    f2 = (row3(norm_ffn2_g), bf(ffn2_w_gate), bf(ffn2_w_up), bf(ffn2_w_down))
    g_mix = row3(norm_mix_g)
    w_in_b, w_ba, w_bc, w_o = bf(w_in), bf(w_branch_attn), bf(w_branch_conv), bf(w_out)
    lam_vecs = jnp.stack([lambda_q1, lambda_k1, lambda_q2, lambda_k2], axis=1)
    sg = row3(subln_g)
    bias = _bias_tiles(rel_bias)

    xf = x.reshape(batch * seq, d)
    to3 = lambda a: a.reshape(batch, seq, ATTN_W)
    for layer in range(depth):
        lambda_init = 0.8 - 0.6 * math.exp(-0.3 * layer)
        xf, q, k, v, convb = _layer_in(xf, f1, g_mix, w_in_b, conv_w, layer, seq)
        attn = _attention(to3(q), to3(k), to3(v), bias, lam_vecs, sg, layer, lambda_init)
        xf = _layer_out(xf, g_mix, attn.reshape(batch * seq, ATTN_W), convb,
                        w_in_b, w_ba, w_bc, w_o, f2, layer)
    return _final_norm(xf, final_g.reshape(1, d)).reshape(batch, seq, d)
```
